```python
import math
import jax, jax.numpy as jnp
from jax import lax
import numpy as np

D_MODEL = 1024
BATCH = 8
SEQ = 4096
DEPTH = 2

D_MIX = 1024
GDN_HEADS = 4
GDN_HEAD_DIM = 128
GDN_WIDTH = 512
GDN_CONV = 4
GDN_CHUNK = 64
DSA_HEADS = 4
DSA_HEAD_DIM = 64
DSA_WIDTH = 256
DSA_Q_RANK = 256
DSA_KV_RANK = 128
IDX_HEADS = 4
IDX_DIM = 64
DSA_TOPK_MAX = 256
Q_BLOCK = 128
CONV_CH = 256
CONV_WIDTH = 31
MEM_LEN = 256
XA_HEADS = 4
XA_HEAD_DIM = 128
XA_WIDTH = 512
D_FF = 4 * D_MODEL
EPS = 1e-6

IN_SPLITS = (3 * GDN_WIDTH, GDN_WIDTH, GDN_HEADS, GDN_HEADS,
             DSA_Q_RANK, DSA_KV_RANK, IDX_DIM, IDX_HEADS, 2 * CONV_CH)
N_IN = 3 * GDN_WIDTH + GDN_WIDTH + 2 * GDN_HEADS + DSA_Q_RANK + DSA_KV_RANK + IDX_DIM + IDX_HEADS + 2 * CONV_CH

kernel_name = "hybrid_gdn_dsa_conformer_block"


def rms_norm(x, g):
    xf = x.astype(jnp.float32)
    y = xf * lax.rsqrt(jnp.mean(xf * xf, axis=-1, keepdims=True) + EPS)
    return (y * g.astype(jnp.float32)).astype(x.dtype)


def layer_norm(x, g, b):
    xf = x.astype(jnp.float32)
    mu = jnp.mean(xf, axis=-1, keepdims=True)
    xc = xf - mu
    var = jnp.mean(xc * xc, axis=-1, keepdims=True)
    y = xc * lax.rsqrt(var + EPS) * g.astype(jnp.float32) + b.astype(jnp.float32)
    return y.astype(x.dtype)


def l2_norm(x):
    return x * lax.rsqrt(jnp.sum(x * x, axis=-1, keepdims=True) + EPS)


def split_cols(t, sizes):
    out = []
    off = 0
    for n in sizes:
        out.append(t[..., off:off + n])
        off += n
    return out


def causal_depthwise_conv(x, w):
    k, c = w.shape
    return lax.conv_general_dilated(
        x, w[:, None, :].astype(x.dtype), window_strides=(1,), padding=((k - 1, 0),),
        dimension_numbers=("NWC", "WIO", "NWC"), feature_group_count=c)


def gated_delta_rule(q, k, v, g, beta):
    b, s, h, dk = q.shape
    dv = v.shape[-1]
    c = GDN_CHUNK
    n = s // c
    q = q * (dk ** -0.5)

    def chunk(t):
        return jnp.moveaxis(t.reshape((b, n, c) + t.shape[2:]), 3, 2)

    qc, kc, vc = chunk(q), chunk(k), chunk(v)
    gc, bc = chunk(g), chunk(beta)
    decay = jnp.cumsum(gc, axis=-1)
    tril = jnp.tril(jnp.ones((c, c), bool))
    strict = jnp.tril(jnp.ones((c, c), bool), -1)
    diff = decay[..., :, None] - decay[..., None, :]
    gamma = jnp.where(tril, jnp.exp(jnp.where(tril, diff, 0.0)), 0.0)
    kb = kc * bc[..., None]
    a_mat = jnp.where(strict, jnp.einsum("bnhid,bnhjd->bnhij", kb, kc) * gamma, 0.0)
    lhs = a_mat + jnp.eye(c, dtype=a_mat.dtype)
    rhs = jnp.concatenate([vc * bc[..., None], kb * jnp.exp(decay)[..., None]], axis=-1)
    sol = lax.linalg.triangular_solve(lhs, rhs, left_side=True, lower=True, unit_diagonal=True)
    u, w = sol[..., :dv], sol[..., dv:]
    qk = jnp.where(tril, jnp.einsum("bnhid,bnhjd->bnhij", qc, kc) * gamma, 0.0)

    def step(state, inp):
        q_i, k_i, u_i, w_i, qk_i, dec_i = inp
        v_new = u_i - jnp.einsum("bhcd,bhde->bhce", w_i, state)
        o = (jnp.einsum("bhcd,bhde->bhce", q_i * jnp.exp(dec_i)[..., None], state)
             + jnp.einsum("bhij,bhje->bhie", qk_i, v_new))
        last = dec_i[..., -1]
        k_dec = k_i * jnp.exp(last[..., None] - dec_i)[..., None]
        state = state * jnp.exp(last)[..., None, None] + jnp.einsum("bhcd,bhce->bhde", k_dec, v_new)
        return state, o

    xs = tuple(jnp.moveaxis(t, 1, 0) for t in (qc, kc, u, w, qk, decay))
    _, o = lax.scan(step, jnp.zeros((b, h, dk, dv), jnp.float32), xs)
    o = jnp.swapaxes(jnp.moveaxis(o, 0, 1), 2, 3)
    return o.reshape(b, s, h, dv)


def gdn_mixer(qkv, z, a_g, b_g, conv_w, a_log, dt_bias, norm_g):
    b, s, _ = qkv.shape
    dt = qkv.dtype
    qkv = jax.nn.silu(causal_depthwise_conv(qkv, conv_w)).astype(jnp.float32)
    q, k, v = jnp.split(qkv, 3, axis=-1)
    shp = (b, s, GDN_HEADS, GDN_HEAD_DIM)
    q = l2_norm(q.reshape(shp))
    k = l2_norm(k.reshape(shp))
    v = v.reshape(shp)
    g = -jnp.exp(a_log.astype(jnp.float32)) * jax.nn.softplus(a_g.astype(jnp.float32) + dt_bias.astype(jnp.float32))
    beta = jax.nn.sigmoid(b_g.astype(jnp.float32))
    o = gated_delta_rule(q, k, v, g, beta)
    o = rms_norm(o, norm_g) * jax.nn.silu(z.astype(jnp.float32).reshape(shp))
    return o.reshape(b, s, GDN_WIDTH).astype(dt)


def dsa_mixer(c_q, c_kv, k_idx, w_idx, w_qb, w_qi, w_kvb, q_norm_g, k_norm_g):
    b, s, _ = c_q.shape
    q = rms_norm((c_q @ w_qb).reshape(b, s, DSA_HEADS, DSA_HEAD_DIM), q_norm_g)
    kv = c_kv @ w_kvb
    k = rms_norm(kv[..., :DSA_HEAD_DIM], k_norm_g)
    v = kv[..., DSA_HEAD_DIM:]
    q_idx = (c_q @ w_qi).reshape(b, s, IDX_HEADS, IDX_DIM)
    w_idx = w_idx.astype(jnp.float32) * (IDX_HEADS ** -0.5 * IDX_DIM ** -0.5)
    k_idx = k_idx.astype(jnp.float32)
    topk = min(DSA_TOPK_MAX, s // 4)
    nb = s // Q_BLOCK
    pos = jnp.arange(s, dtype=jnp.int32)

    def blk(t):
        return jnp.moveaxis(t.reshape((b, nb, Q_BLOCK) + t.shape[2:]), 1, 0)

    def attend(inp):
        q_b, qi_b, w_b, t_b = inp
        causal = pos[None, :] <= t_b[:, None]
        idx_logits = jnp.einsum("bqhd,bsd->bqhs", qi_b.astype(jnp.float32), k_idx)
        score = jnp.einsum("bqh,bqhs->bqs", w_b, jax.nn.relu(idx_logits))
        score = jnp.where(causal[None], score, -jnp.inf)
        _, sel = lax.top_k(score, topk)
        k_sel = jax.vmap(lambda kk, ii: kk[ii])(k, sel)
        v_sel = jax.vmap(lambda vv, ii: vv[ii])(v, sel)
        valid = sel <= t_b[None, :, None]
        logits = jnp.einsum("bqhd,bqkd->bqhk", q_b.astype(jnp.float32), k_sel.astype(jnp.float32)) * (DSA_HEAD_DIM ** -0.5)
        logits = jnp.where(valid[:, :, None, :], logits, -jnp.inf)
        p = jax.nn.softmax(logits, axis=-1)
        return jnp.einsum("bqhk,bqkd->bqhd", p, v_sel.astype(jnp.float32)).astype(q_b.dtype)

    o = lax.map(attend, (blk(q), blk(q_idx), blk(w_idx), pos.reshape(nb, Q_BLOCK)))
    return jnp.moveaxis(o, 0, 1).reshape(b, s, DSA_WIDTH)


def conv_mixer(ug, dw, dw_b, ln_g, ln_b):
    u, gate = jnp.split(ug, 2, axis=-1)
    y = u * jax.nn.sigmoid(gate)
    y = causal_depthwise_conv(y, dw) + dw_b
    y = layer_norm(y, ln_g, ln_b)
    return jax.nn.silu(y)


def memory_cross_attention(h, mem_n, wq, wkv, q_g, k_g, wo):
    b, s, _ = h.shape
    m = mem_n.shape[1]
    q = rms_norm((h @ wq).reshape(b, s, XA_HEADS, XA_HEAD_DIM), q_g)
    kv = mem_n @ wkv
    k = rms_norm(kv[..., :XA_WIDTH].reshape(b, m, XA_HEADS, XA_HEAD_DIM), k_g)
    v = kv[..., XA_WIDTH:].reshape(b, m, XA_HEADS, XA_HEAD_DIM)
    logits = jnp.einsum("bshd,bmhd->bhsm", q.astype(jnp.float32), k.astype(jnp.float32)) * (XA_HEAD_DIM ** -0.5)
    p = jax.nn.softmax(logits, axis=-1)
    o = jnp.einsum("bhsm,bmhd->bshd", p, v.astype(jnp.float32)).astype(h.dtype)
    return o.reshape(b, s, XA_WIDTH) @ wo


def setup_inputs(seed: int = 0) -> dict:
    key = jax.random.key(seed)
    ks = jax.random.split(key, 32)
    f32 = jnp.float32

    def nrm(k, shape, scale):
        return jax.random.normal(k, shape, f32) * scale

    def gain(k, shape):
        return 1.0 + 0.02 * jax.random.normal(k, shape, f32)

    dt = jnp.exp(jax.random.uniform(ks[5], (DEPTH, GDN_HEADS), f32, math.log(1e-3), math.log(1e-1)))
    dt_bias = dt + jnp.log(-jnp.expm1(-dt))
    return {
        "x": nrm(ks[0], (BATCH, SEQ, D_MODEL), 1.0),
        "mem": nrm(ks[1], (BATCH, MEM_LEN, D_MODEL), 1.0),
        "norm_mix": gain(ks[2], (DEPTH, D_MODEL)),
        "w_in": nrm(ks[3], (DEPTH, D_MODEL, N_IN), D_MODEL ** -0.5),
        "gdn_conv": nrm(ks[4], (DEPTH, GDN_CONV, 3 * GDN_WIDTH), GDN_CONV ** -0.5),
        "gdn_a_log": jnp.log(jax.random.uniform(ks[6], (DEPTH, GDN_HEADS), f32, 1.0, 16.0)),
        "gdn_dt_bias": dt_bias,
        "gdn_norm": gain(ks[7], (DEPTH, GDN_HEAD_DIM)),
        "dsa_w_qb": nrm(ks[8], (DEPTH, DSA_Q_RANK, DSA_WIDTH), DSA_Q_RANK ** -0.5),
        "dsa_w_qi": nrm(ks[9], (DEPTH, DSA_Q_RANK, IDX_HEADS * IDX_DIM), DSA_Q_RANK ** -0.5),
        "dsa_w_kvb": nrm(ks[10], (DEPTH, DSA_KV_RANK, 2 * DSA_HEAD_DIM), DSA_KV_RANK ** -0.5),
        "dsa_q_norm": gain(ks[11], (DEPTH, DSA_HEAD_DIM)),
        "dsa_k_norm": gain(ks[12], (DEPTH, DSA_HEAD_DIM)),
        "conv_dw": nrm(ks[13], (DEPTH, CONV_WIDTH, CONV_CH), CONV_WIDTH ** -0.5),
        "conv_dw_b": nrm(ks[14], (DEPTH, CONV_CH), 0.02),
        "conv_ln_g": gain(ks[15], (DEPTH, CONV_CH)),
        "conv_ln_b": nrm(ks[16], (DEPTH, CONV_CH), 0.02),
        "w_out": nrm(ks[17], (DEPTH, D_MIX, D_MODEL), D_MIX ** -0.5),
        "norm_mem": gain(ks[18], (D_MODEL,)),
        "norm_cross": gain(ks[19], (DEPTH, D_MODEL)),
        "xa_wq": nrm(ks[20], (DEPTH, D_MODEL, XA_WIDTH), D_MODEL ** -0.5),
        "xa_wkv": nrm(ks[21], (DEPTH, D_MODEL, 2 * XA_WIDTH), D_MODEL ** -0.5),
        "xa_q_norm": gain(ks[22], (DEPTH, XA_HEAD_DIM)),
        "xa_k_norm": gain(ks[23], (DEPTH, XA_HEAD_DIM)),
        "xa_wo": nrm(ks[24], (DEPTH, XA_WIDTH, D_MODEL), XA_WIDTH ** -0.5),
        "norm_mlp": gain(ks[25], (DEPTH, D_MODEL)),
        "mlp_w1": nrm(ks[26], (DEPTH, D_MODEL, D_FF), D_MODEL ** -0.5),
        "mlp_w2": nrm(ks[27], (DEPTH, D_FF, D_MODEL), D_FF ** -0.5),
    }


def reference(x, mem, norm_mix, w_in, gdn_conv, gdn_a_log, gdn_dt_bias, gdn_norm,
              dsa_w_qb, dsa_w_qi, dsa_w_kvb, dsa_q_norm, dsa_k_norm,
              conv_dw, conv_dw_b, conv_ln_g, conv_ln_b, w_out,
              norm_mem, norm_cross, xa_wq, xa_wkv, xa_q_norm, xa_k_norm, xa_wo,
              norm_mlp, mlp_w1, mlp_w2):
    mem_n = rms_norm(mem, norm_mem)
    for l in range(DEPTH):
        h = rms_norm(x, norm_mix[l])
        proj = h @ w_in[l]
        qkv, z, a_g, b_g, c_q, c_kv, k_idx, w_idx, ug = split_cols(proj, IN_SPLITS)
        y_a = gdn_mixer(qkv, z, a_g, b_g, gdn_conv[l], gdn_a_log[l], gdn_dt_bias[l], gdn_norm[l])
        y_b = dsa_mixer(c_q, c_kv, k_idx, w_idx, dsa_w_qb[l], dsa_w_qi[l], dsa_w_kvb[l],
                        dsa_q_norm[l], dsa_k_norm[l])
        y_c = conv_mixer(ug, conv_dw[l], conv_dw_b[l], conv_ln_g[l], conv_ln_b[l])
        x = x + jnp.concatenate([y_a, y_b, y_c], axis=-1) @ w_out[l]
        x = x + memory_cross_attention(rms_norm(x, norm_cross[l]), mem_n, xa_wq[l], xa_wkv[l],
                                       xa_q_norm[l], xa_k_norm[l], xa_wo[l])
        hm = rms_norm(x, norm_mlp[l]) @ mlp_w1[l]
        x = x + jnp.square(jax.nn.relu(hm)) @ mlp_w2[l]
    return x
```

```python
import functools

import jax
import jax.numpy as jnp
from jax import lax
from jax.experimental import pallas as pl
from jax.experimental.pallas import tpu as pltpu

F32 = jnp.float32
BF16 = jnp.bfloat16
EPS = 1e-6

D_MODEL = 1024
GDN_HEADS = 4
GDN_HEAD_DIM = 128
GDN_WIDTH = GDN_HEADS * GDN_HEAD_DIM
GDN_CONV = 4
GDN_CHUNK = 128
DSA_HEADS = 4
DSA_HEAD_DIM = 64
DSA_WIDTH = DSA_HEADS * DSA_HEAD_DIM
DSA_Q_RANK = 256
DSA_KV_RANK = 128
IDX_HEADS = 4
IDX_DIM = 64
DSA_TOPK_MAX = 256
CONV_CH = 256
CONV_WIDTH = 31
XA_HEADS = 4
XA_HEAD_DIM = 128
XA_WIDTH = XA_HEADS * XA_HEAD_DIM
D_FF = 4 * D_MODEL

LANES = 128
CONV_HALO = 32
GDN_HALO = 8
VMEM_LIMIT = 56 * 1024 * 1024
INT_MIN = -2 ** 31
_MASKED = -1e30
_ACC_ROWS = 64

_C_QKV = (0, 3 * GDN_WIDTH)
_C_Z = (_C_QKV[1], _C_QKV[1] + GDN_WIDTH)
_C_UG = (_C_Z[1], _C_Z[1] + 2 * CONV_CH)
_C_CKV = (_C_UG[1], _C_UG[1] + DSA_KV_RANK)
_C_SMALL = (_C_CKV[1], _C_CKV[1] + LANES)
_C_KIDX = (_C_SMALL[1], _C_SMALL[1] + IDX_DIM)
N_PACKED = _C_KIDX[1]
_L_AG = 0
_L_BG = GDN_HEADS
_R_CQ = (0, DSA_Q_RANK)
_R_CKV = (_R_CQ[1], _R_CQ[1] + DSA_KV_RANK)
_R_WIDX = (_R_CKV[1], _R_CKV[1] + 8)
N_TRANSPOSED = 512


def _dot(a, b):
    return jnp.dot(a, b, preferred_element_type=F32)


def _dot_nt(a, b):
    return lax.dot_general(a, b, (((1,), (1,)), ((), ())), preferred_element_type=F32)


def _bf(x):
    return x.astype(BF16)


def _split3(x):
    hi = _bf(x)
    r = x - hi.astype(F32)
    mid = _bf(r)
    lo = _bf(r - mid.astype(F32))
    return hi, mid, lo


def _rms_scale(x):
    return x * lax.rsqrt(jnp.mean(x * x, axis=-1, keepdims=True) + EPS)


def _sigmoid(x):
    return 1.0 / (1.0 + jnp.exp(-x))


def _silu(x):
    return x * _sigmoid(x)


def _softplus(x):
    return jnp.maximum(x, 0.0) + jnp.log(1.0 + jnp.exp(-jnp.abs(x)))


def _unit_lower_inverses(mats, row, col):
    n = mats[0].shape[0]

    def same_block(log2_size):
        return lax.shift_right_logical(row, log2_size) == lax.shift_right_logical(col, log2_size)

    eye = jnp.where(row == col, 1.0, 0.0)
    in_pair = same_block(1)
    xs = [eye - jnp.where(in_pair, a, 0.0) for a in mats]
    level = 1
    while (1 << level) < n:
        off_mask = same_block(level + 1) & jnp.logical_not(same_block(level))
        x_bfs = [_bf(x) for x in xs]
        halves = [_bf(_dot(x_bf, _bf(jnp.where(off_mask, a, 0.0)))) for x_bf, a in zip(x_bfs, mats)]
        xs = [x - _dot(half, x_bf) for x, half, x_bf in zip(xs, halves, x_bfs)]
        level += 1
    return xs


def _memkv_kernel(mem_ref, g_ref, wkv_ref, kg_ref, k_ref, v_ref):
    mn = _bf(_rms_scale(mem_ref[0]) * g_ref[...])
    kv = _dot(mn, wkv_ref[0])
    for h in range(XA_HEADS):
        sl = slice(h * XA_HEAD_DIM, (h + 1) * XA_HEAD_DIM)
        k_ref[0, 0, :, sl] = _bf(_rms_scale(kv[:, sl]) * kg_ref[0])
    v_ref[0, 0] = _bf(kv[:, XA_WIDTH:])


def _memkv(mem, norm_mem, wkv, k_g):
    b, m, d = mem.shape
    depth = wkv.shape[0]
    out = jax.ShapeDtypeStruct((depth, b, m, XA_WIDTH), BF16)
    return pl.pallas_call(
        _memkv_kernel,
        out_shape=(out, out),
        grid=(depth, b),
        in_specs=[
            pl.BlockSpec((1, m, d), lambda l, i: (i, 0, 0)),
            pl.BlockSpec((1, d), lambda l, i: (0, 0)),
            pl.BlockSpec((1, d, 2 * XA_WIDTH), lambda l, i: (l, 0, 0)),
            pl.BlockSpec((1, 1, XA_HEAD_DIM), lambda l, i: (l, 0, 0)),
        ],
        out_specs=(
            pl.BlockSpec((1, 1, m, XA_WIDTH), lambda l, i: (l, i, 0, 0)),
            pl.BlockSpec((1, 1, m, XA_WIDTH), lambda l, i: (l, i, 0, 0)),
        ),
        compiler_params=pltpu.CompilerParams(
            dimension_semantics=("arbitrary", "arbitrary"), vmem_limit_bytes=VMEM_LIMIT),
        name="memkv",
    )(mem, norm_mem.reshape(1, d), wkv, k_g.reshape(depth, 1, XA_HEAD_DIM))


def _proj_kernel(x_ref, g_ref, w_ref, wt_ref, wqbt_ref, wqit_ref, wk_ref, wvt_ref, qs_ref,
                 qkv_ref, z_ref, ug_ref, small_ref, kidx_ref, k_ref, qt_ref, qit_ref, vt_ref,
                 widxt_ref):
    h = _bf(_rms_scale(x_ref[0]) * g_ref[...])
    qkv_ref[0] = _bf(_dot(h, w_ref[:, _C_QKV[0]:_C_QKV[1]]))
    z_ref[0] = _bf(_dot(h, w_ref[:, _C_Z[0]:_C_Z[1]]))
    ug_ref[0] = _bf(_dot(h, w_ref[:, _C_UG[0]:_C_UG[1]]))
    small_ref[0] = _dot(h, w_ref[:, _C_SMALL[0]:_C_SMALL[1]])
    kidx_ref[0] = _bf(_dot(h, w_ref[:, _C_KIDX[0]:_C_KIDX[1]]))
    ckv = _bf(_dot(h, w_ref[:, _C_CKV[0]:_C_CKV[1]]))
    k_ref[0] = _bf(_rms_scale(_dot(ckv, wk_ref[...])))
    t = _dot_nt(wt_ref[...], h)
    cq_t = _bf(t[_R_CQ[0]:_R_CQ[1]])
    ckv_t = _bf(t[_R_CKV[0]:_R_CKV[1]])
    widxt_ref[0] = t[_R_WIDX[0]:_R_WIDX[1]]
    for hh in range(DSA_HEADS):
        qt = _dot(wqbt_ref[hh], cq_t)
        qt = _bf(qt * lax.rsqrt(jnp.mean(qt * qt, axis=0, keepdims=True) + EPS) * qs_ref[...])
        qit = _bf(_dot(wqit_ref[hh], cq_t))
        for blk in range(qt_ref.shape[1]):
            qt_ref[0, blk, :, hh * LANES:(hh + 1) * LANES] = qt[:, blk * LANES:(blk + 1) * LANES]
            qit_ref[0, blk, :, hh * LANES:(hh + 1) * LANES] = qit[:, blk * LANES:(blk + 1) * LANES]
    vt_ref[0, 0] = _bf(_dot(wvt_ref[...], ckv_t))


def _proj(x, g, w, wt, wqbt, wqit, wk, wvt, qs, tm):
    b, s, d = x.shape
    nt = s // tm
    tok = lambda n, dt=BF16: jax.ShapeDtypeStruct((b, s, n), dt)
    tok_spec = lambda n: pl.BlockSpec((1, tm, n), lambda i, j: (i, j, 0))
    full = lambda a: pl.BlockSpec(a.shape, lambda i, j, _n=a.ndim: (0,) * _n)
    out_shape = (
        tok(3 * GDN_WIDTH), tok(GDN_WIDTH), tok(2 * CONV_CH), tok(LANES, F32),
        tok(IDX_DIM), tok(DSA_HEAD_DIM),
        jax.ShapeDtypeStruct((b, s // LANES, DSA_HEAD_DIM, DSA_HEADS * LANES), BF16),
        jax.ShapeDtypeStruct((b, s // LANES, IDX_DIM, IDX_HEADS * LANES), BF16),
        jax.ShapeDtypeStruct((b, nt, DSA_HEAD_DIM, tm), BF16),
        jax.ShapeDtypeStruct((b, 8, s), F32),
    )
    head_spec = pl.BlockSpec((1, tm // LANES, DSA_HEAD_DIM, DSA_HEADS * LANES),
                             lambda i, j: (i, j, 0, 0))
    out_specs = (
        tok_spec(3 * GDN_WIDTH), tok_spec(GDN_WIDTH), tok_spec(2 * CONV_CH), tok_spec(LANES),
        tok_spec(IDX_DIM), tok_spec(DSA_HEAD_DIM), head_spec, head_spec,
        pl.BlockSpec((1, 1, DSA_HEAD_DIM, tm), lambda i, j: (i, j, 0, 0)),
        pl.BlockSpec((1, 8, tm), lambda i, j: (i, 0, j)),
    )
    return pl.pallas_call(
        _proj_kernel,
        out_shape=out_shape,
        grid=(b, nt),
        in_specs=[tok_spec(d), full(g), full(w), full(wt), full(wqbt), full(wqit), full(wk),
                  full(wvt), full(qs)],
        out_specs=out_specs,
        compiler_params=pltpu.CompilerParams(
            dimension_semantics=("parallel", "parallel"), vmem_limit_bytes=VMEM_LIMIT),
        name="proj",
    )(x, g, w, wt, wqbt, wqit, wk, wvt, qs)


def _gdn_kernel(qkv_ref, z_ref, small_ref, cw_ref, alog_ref, dt_ref, ng_ref, ltri_ref, y_ref,
                state_ref, halo_ref, cbuf_ref):
    c = GDN_CHUNK
    hd = GDN_HEAD_DIM
    ts = qkv_ref.shape[1]

    @pl.when(pl.program_id(1) == 0)
    def _():
        state_ref[...] = jnp.zeros_like(state_ref)
        halo_ref[...] = jnp.zeros_like(halo_ref)

    xin = qkv_ref[0].astype(F32)
    cbuf_ref[0:GDN_HALO, :] = halo_ref[...]
    cbuf_ref[GDN_HALO:GDN_HALO + ts, :] = xin
    halo_ref[...] = xin[ts - GDN_HALO:ts, :]
    base = GDN_HALO - (GDN_CONV - 1)
    acc = cw_ref[0:1, :] * cbuf_ref[base:base + ts, :]
    for j in range(1, GDN_CONV):
        acc = acc + cw_ref[j:j + 1, :] * cbuf_ref[base + j:base + j + ts, :]
    conv = _silu(acc)

    sm = small_ref[0]
    gate = -jnp.exp(alog_ref[...]) * _softplus(sm + dt_ref[...])
    beta_all = _sigmoid(sm)
    ltri = ltri_ref[...]

    row = lax.broadcasted_iota(jnp.int32, (c, c), 0)
    col = lax.broadcasted_iota(jnp.int32, (c, c), 1)
    tril = row >= col
    strict = row > col

    n_chunks = ts // c
    probs = [(ci, h) for ci in range(n_chunks) for h in range(GDN_HEADS)]
    decs, dec_ts = [], []
    for ci in range(n_chunks):
        g_hi, g_mid, g_lo = _split3(gate[ci * c:(ci + 1) * c])
        dec = _dot(ltri, g_hi) + _dot(ltri, g_mid) + _dot(ltri, g_lo)
        decs.append(dec)
        dec_ts.append(dec.T)

    def head_cols(ci, h, section):
        lo = section * GDN_WIDTH + h * hd
        return conv[ci * c:(ci + 1) * c, lo:lo + hd]

    qs, ks, kbs, vbs, d_cols, lasts, gammas = [], [], [], [], [], [], []
    for ci, h in probs:
        q = head_cols(ci, h, 0)
        k = head_cols(ci, h, 1)
        v = head_cols(ci, h, 2)
        q = q * lax.rsqrt(jnp.sum(q * q, axis=-1, keepdims=True) + EPS) * (hd ** -0.5)
        k = k * lax.rsqrt(jnp.sum(k * k, axis=-1, keepdims=True) + EPS)
        d_col = decs[ci][:, _L_AG + h:_L_AG + h + 1]
        d_row = dec_ts[ci][_L_AG + h:_L_AG + h + 1, :]
        beta = beta_all[ci * c:(ci + 1) * c, _L_BG + h:_L_BG + h + 1]
        qs.append(q)
        ks.append(k)
        kbs.append(k * beta)
        vbs.append(v * beta)
        d_cols.append(d_col)
        lasts.append(decs[ci][c - 1:c, _L_AG + h:_L_AG + h + 1])
        gammas.append(jnp.where(tril, jnp.exp(jnp.where(tril, d_col - d_row, 0.0)), 0.0))
    k_bfs = [_bf(k) for k in ks]
    a_mats = [jnp.where(strict, _dot_nt(_bf(kb), k_bf) * g, 0.0)
              for kb, k_bf, g in zip(kbs, k_bfs, gammas)]
    qks = [_bf(jnp.where(tril, _dot_nt(_bf(q), k_bf) * g, 0.0)) for q, k_bf, g in zip(qs, k_bfs, gammas)]
    t_mats = _unit_lower_inverses(a_mats, row, col)
    sols = [_dot(_bf(t), _bf(jnp.concatenate([vb, kb * jnp.exp(d)], axis=1)))
            for t, vb, kb, d in zip(t_mats, vbs, kbs, d_cols)]
    us = [sol[:, :hd] for sol in sols]
    ws = [_bf(sol[:, hd:]) for sol in sols]
    q_decs = [_bf(q * jnp.exp(d)) for q, d in zip(qs, d_cols)]
    k_dec_ts = [_bf((k * jnp.exp(last - d)).T) for k, last, d in zip(ks, lasts, d_cols)]
    decays = [jnp.exp(last) for last in lasts]

    states = [state_ref[h] for h in range(GDN_HEADS)]
    for ci in range(n_chunks):
        idx = [ci * GDN_HEADS + h for h in range(GDN_HEADS)]
        s_bfs = [_bf(s) for s in states]
        v_news = [_bf(us[i] - _dot(ws[i], s_bf)) for i, s_bf in zip(idx, s_bfs)]
        outs = [_dot(q_decs[i], s_bf) + _dot(qks[i], v_new)
                for i, s_bf, v_new in zip(idx, s_bfs, v_news)]
        states = [s * decays[i] + _dot(k_dec_ts[i], v_new)
                  for i, s, v_new in zip(idx, states, v_news)]
        for h, o in enumerate(outs):
            zh = z_ref[0, ci * c:(ci + 1) * c, h * hd:(h + 1) * hd].astype(F32)
            y_ref[0, ci * c:(ci + 1) * c, h * hd:(h + 1) * hd] = _bf(
                _rms_scale(o) * ng_ref[...] * _silu(zh))
    for h in range(GDN_HEADS):
        state_ref[h] = states[h]


def _gdn(qkv, z, small, conv_w, alog_row, dt_row, norm_g, ts):
    b, s, _ = qkv.shape
    c = GDN_CHUNK
    ltri = jnp.tril(jnp.ones((c, c), F32)).astype(BF16)
    tok_spec = lambda n: pl.BlockSpec((1, ts, n), lambda i, j: (i, j, 0))
    full = lambda a: pl.BlockSpec(a.shape, lambda i, j, _n=a.ndim: (0,) * _n)
    return pl.pallas_call(
        _gdn_kernel,
        out_shape=jax.ShapeDtypeStruct((b, s, GDN_WIDTH), BF16),
        grid=(b, s // ts),
        in_specs=[tok_spec(3 * GDN_WIDTH), tok_spec(GDN_WIDTH), tok_spec(LANES),
                  full(conv_w), full(alog_row), full(dt_row), full(norm_g), full(ltri)],
        out_specs=tok_spec(GDN_WIDTH),
        scratch_shapes=[
            pltpu.VMEM((GDN_HEADS, GDN_HEAD_DIM, GDN_HEAD_DIM), F32),
            pltpu.VMEM((GDN_HALO, 3 * GDN_WIDTH), F32),
            pltpu.VMEM((GDN_HALO + ts, 3 * GDN_WIDTH), F32),
        ],
        compiler_params=pltpu.CompilerParams(
            dimension_semantics=("parallel", "arbitrary"), vmem_limit_bytes=VMEM_LIMIT),
        name="gdn",
    )(qkv, z, small, conv_w, alog_row, dt_row, norm_g, ltri)


def _ordinal_to_float(u):
    key = u ^ jnp.int32(INT_MIN)
    bits = key ^ (lax.shift_right_arithmetic(key, 31) & jnp.int32(0x7FFFFFFF))
    f = lax.bitcast_convert_type(bits, F32)
    return jnp.where((bits < 0) & (f != f), -jnp.inf, f)


def _truncate_to_bf16(x):
    bits = lax.bitcast_convert_type(x, jnp.int32) & jnp.int32(-65536)
    return lax.bitcast_convert_type(bits, F32)


def _dsa_kernel(qt_ref, qit_ref, widxt_ref, kidx_ref, k_ref, vt_ref, ltri_ref, y_ref, score_ref,
                score_hi_ref, *, topk, n_slabs):
    tq = LANES
    kc = vt_ref.shape[3]
    last = n_slabs - 1
    q0 = (last * (kc // tq) + pl.program_id(1)) * tq
    w_all = widxt_ref[0] * (IDX_HEADS ** -0.5 * IDX_DIM ** -0.5)

    def slab_rows(c):
        return slice(c * kc, (c + 1) * kc)

    def head(x, h):
        return x[:, h * tq:(h + 1) * tq]

    def fold(x, op):
        part = op(x.reshape(x.shape[0] // _ACC_ROWS, _ACC_ROWS, x.shape[1]), axis=0)
        return op(part, axis=0, keepdims=True)

    qit = qit_ref[0, 0]
    for c in range(n_slabs):
        lg = _dot(kidx_ref[0, slab_rows(c), :], qit)
        sc = jnp.zeros((kc, tq), F32)
        for h in range(IDX_HEADS):
            sc = sc + w_all[h:h + 1, :] * jnp.maximum(head(lg, h), 0.0)
        if c == last:
            kpos = last * kc + lax.broadcasted_iota(jnp.int32, (kc, tq), 0)
            qpos = q0 + lax.broadcasted_iota(jnp.int32, (kc, tq), 1)
            sc = jnp.where(kpos <= qpos, sc, -jnp.inf)
        score_ref[c] = sc
        score_hi_ref[c] = _bf(_truncate_to_bf16(sc))

    kf = jnp.float32(topk)

    def count(pred):
        acc = jnp.zeros((_ACC_ROWS, tq), F32)
        for c in range(n_slabs):
            m = jnp.where(pred(score_ref[c]), 1.0, 0.0)
            acc = acc + jnp.sum(m.reshape(kc // _ACC_ROWS, _ACC_ROWS, tq), axis=0)
        return jnp.sum(acc, axis=0, keepdims=True)

    def count_hi(cand_hi):
        one, zero = jnp.ones((), BF16), jnp.zeros((), BF16)
        acc = jnp.zeros((_ACC_ROWS, tq), BF16)
        for c in range(n_slabs):
            m = jnp.where(score_hi_ref[c] >= cand_hi, one, zero)
            for r in range(0, kc, _ACC_ROWS):
                acc = acc + m[r:r + _ACC_ROWS]
        return jnp.sum(acc.astype(F32), axis=0, keepdims=True)

    def thr_hi_body(i, t_u):
        cand_u = t_u | lax.shift_left(jnp.int32(1), 31 - i)
        cnt = count_hi(_bf(_truncate_to_bf16(_ordinal_to_float(cand_u))))
        return jnp.where(cnt >= kf, cand_u, t_u)

    def thr_body(i, t_u):
        cand_u = t_u | lax.shift_left(jnp.int32(1), 31 - i)
        cand = _ordinal_to_float(cand_u)
        cnt = count(lambda sc: sc >= cand)
        return jnp.where(cnt >= kf, cand_u, t_u)

    assert n_slabs * kc // _ACC_ROWS <= 256
    t_u = lax.fori_loop(0, 16, thr_hi_body, jnp.zeros((1, tq), jnp.int32))
    t_u = lax.fori_loop(16, 32, thr_body, t_u)
    thr = _ordinal_to_float(t_u)
    need = kf - count(lambda sc: sc > thr)
    thr_tie = jnp.where(thr > -jnp.inf, thr, jnp.nan)

    ltri = ltri_ref[...]
    sub = ltri.shape[0]
    neg = jnp.float32(_MASKED)

    seen = jnp.zeros((1, tq), F32)
    for c in range(n_slabs):
        for part in range(kc // sub):
            sc = score_ref[c, part * sub:(part + 1) * sub, :]
            tie = jnp.where(sc == thr_tie, 1.0, 0.0)
            rank = seen + _dot(ltri, _bf(tie))
            take = jnp.where(sc > thr, 1.0, jnp.where(rank <= need, tie, 0.0))
            score_ref[c, part * sub:(part + 1) * sub, :] = jnp.where(take > 0.5, 0.0, neg)
            seen = seen + jnp.sum(tie, axis=0, keepdims=True)

    qt = qt_ref[0, 0]
    m_run = jnp.full((1, DSA_HEADS * tq), neg, F32)
    l_run = jnp.zeros((1, DSA_HEADS * tq), F32)
    acc = jnp.zeros((DSA_HEAD_DIM, DSA_HEADS * tq), F32)
    for c in range(n_slabs):
        mask = score_ref[c]
        lg = _dot(k_ref[0, slab_rows(c), :], qt)
        lg = jnp.concatenate([head(lg, h) + mask for h in range(DSA_HEADS)], axis=1)
        m_new = jnp.maximum(m_run, fold(lg, jnp.max))
        alpha = jnp.exp(m_run - m_new)
        p = jnp.exp(lg - m_new)
        l_run = alpha * l_run + fold(p, jnp.sum)
        acc = alpha * acc + _dot(vt_ref[0, c], _bf(p))
        m_run = m_new
    out = acc * (1.0 / l_run)
    out_t = jnp.concatenate([head(out, h) for h in range(DSA_HEADS)], axis=0)
    y_ref[0] = _bf(out_t.T)


def _dsa(qt, qit, widxt, kidx, k, vt, tq):
    b, s, _ = k.shape
    total_slabs, kc = vt.shape[1], vt.shape[3]
    assert kc % _ACC_ROWS == 0 and kc % tq == 0 and tq == LANES
    topk = min(DSA_TOPK_MAX, s // 4)
    sub = min(kc, LANES)
    ltri = jnp.tril(jnp.ones((sub, sub), F32)).astype(BF16)
    per_group = kc // tq
    outs = []
    for n in range(1, total_slabs + 1):
        first = (n - 1) * per_group
        head_spec = pl.BlockSpec((1, 1, DSA_HEAD_DIM, DSA_HEADS * LANES),
                                 lambda i, j, _f=first: (i, _f + j, 0, 0))
        outs.append(pl.pallas_call(
            functools.partial(_dsa_kernel, topk=topk, n_slabs=n),
            out_shape=jax.ShapeDtypeStruct((b, kc, DSA_WIDTH), BF16),
            grid=(b, per_group),
            in_specs=[
                head_spec, head_spec,
                pl.BlockSpec((1, 8, tq), lambda i, j, _f=first: (i, 0, _f + j)),
                pl.BlockSpec((1, n * kc, IDX_DIM), lambda i, j: (i, 0, 0)),
                pl.BlockSpec((1, n * kc, DSA_HEAD_DIM), lambda i, j: (i, 0, 0)),
                pl.BlockSpec((1, n, DSA_HEAD_DIM, kc), lambda i, j: (i, 0, 0, 0)),
                pl.BlockSpec((sub, sub), lambda i, j: (0, 0)),
            ],
            out_specs=pl.BlockSpec((1, tq, DSA_WIDTH), lambda i, j: (i, j, 0)),
            scratch_shapes=[pltpu.VMEM((n, kc, tq), F32), pltpu.VMEM((n, kc, tq), BF16)],
            compiler_params=pltpu.CompilerParams(
                dimension_semantics=("parallel", "parallel"), vmem_limit_bytes=VMEM_LIMIT),
            name=f"dsa{n}",
        )(qt, qit, widxt, kidx, k, vt, ltri))
    return jnp.concatenate(outs, axis=1)


def _cconv_kernel(ug_ref, dw_ref, b_ref, lg_ref, lb_ref, y_ref, halo_ref, cbuf_ref, shift_ref):
    ts = ug_ref.shape[1]
    sl = 8

    @pl.when(pl.program_id(1) == 0)
    def _():
        halo_ref[...] = jnp.zeros_like(halo_ref)

    ug = ug_ref[0].astype(F32)
    glu = ug[:, :CONV_CH] * _sigmoid(ug[:, CONV_CH:])
    cbuf_ref[0:CONV_HALO, :] = halo_ref[...]
    cbuf_ref[CONV_HALO:CONV_HALO + ts, :] = glu
    halo_ref[...] = glu[ts - CONV_HALO:ts, :]
    span = shift_ref.shape[1]
    for r in range(1, sl):
        shift_ref[r - 1] = cbuf_ref[r:r + span, :]
    base = CONV_HALO - (CONV_WIDTH - 1)
    acc = None
    for j in range(CONV_WIDTH):
        off = base + j
        r, a = off % sl, (off // sl) * sl
        win = cbuf_ref[a:a + ts, :] if r == 0 else shift_ref[r - 1, a:a + ts, :]
        term = dw_ref[j:j + 1, :] * win
        acc = term if acc is None else acc + term
    acc = acc + b_ref[...]
    mu = jnp.mean(acc, axis=-1, keepdims=True)
    xc = acc - mu
    var = jnp.mean(xc * xc, axis=-1, keepdims=True)
    y = xc * lax.rsqrt(var + EPS) * lg_ref[...] + lb_ref[...]
    y_ref[0] = _bf(_silu(y))


def _cconv(ug, dw, dw_b, ln_g, ln_b, ts):
    b, s, _ = ug.shape
    full = lambda a: pl.BlockSpec(a.shape, lambda i, j, _n=a.ndim: (0,) * _n)
    return pl.pallas_call(
        _cconv_kernel,
        out_shape=jax.ShapeDtypeStruct((b, s, CONV_CH), BF16),
        grid=(b, s // ts),
        in_specs=[pl.BlockSpec((1, ts, 2 * CONV_CH), lambda i, j: (i, j, 0)),
                  full(dw), full(dw_b), full(ln_g), full(ln_b)],
        out_specs=pl.BlockSpec((1, ts, CONV_CH), lambda i, j: (i, j, 0)),
        scratch_shapes=[pltpu.VMEM((CONV_HALO, CONV_CH), F32),
                        pltpu.VMEM((CONV_HALO + ts, CONV_CH), F32),
                        pltpu.VMEM((7, CONV_HALO + ts - 8, CONV_CH), F32)],
        compiler_params=pltpu.CompilerParams(
            dimension_semantics=("parallel", "arbitrary"), vmem_limit_bytes=VMEM_LIMIT),
        name="cconv",
    )(ug, dw, dw_b, ln_g, ln_b)


def _post_kernel(x_ref, ya_ref, yb_ref, yc_ref, wout_ref, gc_ref, wq_ref, qg_ref, km_ref, vm_ref,
                 wo_ref, gm_ref, w1_ref, w2_ref, o_ref):
    x = x_ref[0]
    x = x + _dot(ya_ref[0], wout_ref[0:GDN_WIDTH, :])
    x = x + _dot(yb_ref[0], wout_ref[GDN_WIDTH:GDN_WIDTH + DSA_WIDTH, :])
    x = x + _dot(yc_ref[0], wout_ref[GDN_WIDTH + DSA_WIDTH:, :])

    hc = _bf(_rms_scale(x) * gc_ref[...])
    qx = _dot(hc, wq_ref[...])
    heads = []
    for h in range(XA_HEADS):
        sl = slice(h * XA_HEAD_DIM, (h + 1) * XA_HEAD_DIM)
        qh = _rms_scale(qx[:, sl]) * qg_ref[...] * (XA_HEAD_DIM ** -0.5)
        logits = _dot_nt(_bf(qh), km_ref[0, 0, :, sl])
        m = jnp.max(logits, axis=-1, keepdims=True)
        e = jnp.exp(logits - m)
        p = e * (1.0 / jnp.sum(e, axis=-1, keepdims=True))
        heads.append(_bf(_dot(_bf(p), vm_ref[0, 0, :, sl])))
    x = x + _dot(jnp.concatenate(heads, axis=1), wo_ref[...])

    hm = _bf(_rms_scale(x) * gm_ref[...])
    ff_chunk = D_MODEL
    acc = jnp.zeros_like(x)
    for cidx in range(D_FF // ff_chunk):
        a = jnp.maximum(_dot(hm, w1_ref[:, cidx * ff_chunk:(cidx + 1) * ff_chunk]), 0.0)
        acc = acc + _dot(_bf(a * a), w2_ref[cidx * ff_chunk:(cidx + 1) * ff_chunk, :])
    o_ref[0] = x + acc


def _post(x, ya, yb, yc, wout, g_cross, wq, q_g, kmem, vmem, layer, wo, g_mlp, w1, w2, tm):
    b, s, d = x.shape
    m = kmem.shape[2]
    tok_spec = lambda n: pl.BlockSpec((1, tm, n), lambda i, j: (i, j, 0))
    full = lambda a: pl.BlockSpec(a.shape, lambda i, j, _n=a.ndim: (0,) * _n,
                                  pipeline_mode=pl.Buffered(1))
    mem_spec = pl.BlockSpec((1, 1, m, XA_WIDTH), lambda i, j: (layer, i, 0, 0))
    return pl.pallas_call(
        _post_kernel,
        out_shape=jax.ShapeDtypeStruct((b, s, d), F32),
        grid=(b, s // tm),
        in_specs=[tok_spec(d), tok_spec(GDN_WIDTH), tok_spec(DSA_WIDTH), tok_spec(CONV_CH),
                  full(wout), full(g_cross), full(wq), full(q_g), mem_spec, mem_spec,
                  full(wo), full(g_mlp), full(w1), full(w2)],
        out_specs=tok_spec(d),
        compiler_params=pltpu.CompilerParams(
            dimension_semantics=("parallel", "parallel"), vmem_limit_bytes=VMEM_LIMIT),
        name="post",
    )(x, ya, yb, yc, wout, g_cross, wq, q_g, kmem, vmem, wo, g_mlp, w1, w2)


def _pack_w_in(w_in):
    depth, d, _ = w_in.shape
    sizes = (3 * GDN_WIDTH, GDN_WIDTH, GDN_HEADS, GDN_HEADS, DSA_Q_RANK, DSA_KV_RANK, IDX_DIM,
             IDX_HEADS, 2 * CONV_CH)
    parts, off = [], 0
    for n in sizes:
        parts.append(w_in[..., off:off + n])
        off += n
    qkv, z, a_g, b_g, c_q, c_kv, k_idx, w_idx, ug = parts
    small = jnp.concatenate(
        [a_g, b_g, jnp.zeros((depth, d, LANES - 2 * GDN_HEADS), F32)], axis=-1)
    packed = jnp.concatenate([qkv, z, ug, c_kv, small, k_idx], axis=-1).astype(BF16)
    tr = jnp.concatenate(
        [c_q, c_kv, w_idx,
         jnp.zeros((depth, d, N_TRANSPOSED - DSA_Q_RANK - DSA_KV_RANK - IDX_HEADS), F32)], axis=-1)
    return packed, jnp.swapaxes(tr, 1, 2).astype(BF16)


def _lane_row(vals, offset):
    depth, n = vals.shape
    row = jnp.zeros((depth, 1, LANES), F32)
    return row.at[:, 0, offset:offset + n].set(vals.astype(F32))


def _seq_tile(s, want):
    return want if s % want == 0 else s


def kernel(x, mem, norm_mix, w_in, gdn_conv, gdn_a_log, gdn_dt_bias, gdn_norm, dsa_w_qb, dsa_w_qi,
           dsa_w_kvb, dsa_q_norm, dsa_k_norm, conv_dw, conv_dw_b, conv_ln_g, conv_ln_b, w_out,
           norm_mem, norm_cross, xa_wq, xa_wkv, xa_q_norm, xa_k_norm, xa_wo, norm_mlp, mlp_w1,
           mlp_w2):
    depth = w_in.shape[0]
    b, s, d = x.shape
    assert d == D_MODEL and s % GDN_CHUNK == 0
    tm = _seq_tile(s, 512)
    tq = _seq_tile(s, 128)

    w_packed, w_tr = _pack_w_in(w_in)
    wqbt = jnp.transpose(dsa_w_qb.reshape(depth, DSA_Q_RANK, DSA_HEADS, DSA_HEAD_DIM), (0, 2, 3, 1)).astype(BF16)
    wqit = jnp.transpose(dsa_w_qi.reshape(depth, DSA_Q_RANK, IDX_HEADS, IDX_DIM), (0, 2, 3, 1)).astype(BF16)
    wk = dsa_w_kvb[:, :, :DSA_HEAD_DIM].astype(BF16)
    wvt = jnp.swapaxes(dsa_w_kvb[:, :, DSA_HEAD_DIM:], 1, 2).astype(BF16)
    q_scale = (dsa_q_norm * dsa_k_norm * (DSA_HEAD_DIM ** -0.5)).astype(F32)[:, :, None]
    alog_row = _lane_row(gdn_a_log, _L_AG)
    dt_row = _lane_row(gdn_dt_bias, _L_AG)

    kmem, vmem = _memkv(mem, norm_mem, xa_wkv.astype(BF16), xa_k_norm)

    for l in range(depth):
        qkv, z, ug, small, kidx, k, qt, qit, vt, widxt = _proj(
            x, norm_mix[l][None, :], w_packed[l], w_tr[l], wqbt[l], wqit[l], wk[l], wvt[l],
            q_scale[l], tm)
        y_a = _gdn(qkv, z, small, gdn_conv[l], alog_row[l], dt_row[l], gdn_norm[l][None, :], tm)
        y_b = _dsa(qt, qit, widxt, kidx, k, vt, tq)
        y_c = _cconv(ug, conv_dw[l], conv_dw_b[l][None, :], conv_ln_g[l][None, :],
                     conv_ln_b[l][None, :], tm)
        x = _post(x, y_a, y_b, y_c, w_out[l].astype(BF16), norm_cross[l][None, :],
                  xa_wq[l].astype(BF16), xa_q_norm[l][None, :], kmem, vmem, l,
                  xa_wo[l].astype(BF16), norm_mlp[l][None, :], mlp_w1[l].astype(BF16),
                  mlp_w2[l].astype(BF16), tm)
    return x
```

```python
import functools

import jax
import jax.numpy as jnp
from jax import lax
from jax.experimental import pallas as pl
from jax.experimental.pallas import tpu as pltpu

F32 = jnp.float32
BF16 = jnp.bfloat16
EPS = 1e-6

D_MODEL = 1024
GDN_HEADS = 4
GDN_HEAD_DIM = 128
GDN_WIDTH = GDN_HEADS * GDN_HEAD_DIM
GDN_CONV = 4
GDN_CHUNK = 128
DSA_HEADS = 4
DSA_HEAD_DIM = 64
DSA_WIDTH = DSA_HEADS * DSA_HEAD_DIM
DSA_Q_RANK = 256
DSA_KV_RANK = 128
IDX_HEADS = 4
IDX_DIM = 64
DSA_TOPK_MAX = 256
CONV_CH = 256
CONV_WIDTH = 31
XA_HEADS = 4
XA_HEAD_DIM = 128
XA_WIDTH = XA_HEADS * XA_HEAD_DIM
D_FF = 4 * D_MODEL

LANES = 128
CONV_HALO = 32
GDN_HALO = 8
VMEM_LIMIT = 56 * 1024 * 1024
INT_MIN = -2 ** 31
_MASKED = -1e30
_ACC_ROWS = 64

_C_QKV = (0, 3 * GDN_WIDTH)
_C_Z = (_C_QKV[1], _C_QKV[1] + GDN_WIDTH)
_C_UG = (_C_Z[1], _C_Z[1] + 2 * CONV_CH)
_C_CKV = (_C_UG[1], _C_UG[1] + DSA_KV_RANK)
_C_SMALL = (_C_CKV[1], _C_CKV[1] + LANES)
_C_KIDX = (_C_SMALL[1], _C_SMALL[1] + IDX_DIM)
N_PACKED = _C_KIDX[1]
_L_AG = 0
_L_BG = GDN_HEADS
_R_CQ = (0, DSA_Q_RANK)
_R_CKV = (_R_CQ[1], _R_CQ[1] + DSA_KV_RANK)
_R_WIDX = (_R_CKV[1], _R_CKV[1] + 8)
N_TRANSPOSED = 512


def _dot(a, b):
    return jnp.dot(a, b, preferred_element_type=F32)


def _dot_nt(a, b):
    return lax.dot_general(a, b, (((1,), (1,)), ((), ())), preferred_element_type=F32)


def _bf(x):
    return x.astype(BF16)


def _split3(x):
    hi = _bf(x)
    r = x - hi.astype(F32)
    mid = _bf(r)
    lo = _bf(r - mid.astype(F32))
    return hi, mid, lo


def _rms_scale(x):
    return x * lax.rsqrt(jnp.mean(x * x, axis=-1, keepdims=True) + EPS)


def _sigmoid(x):
    return 1.0 / (1.0 + jnp.exp(-x))


def _silu(x):
    return x * _sigmoid(x)


def _softplus(x):
    return jnp.maximum(x, 0.0) + jnp.log(1.0 + jnp.exp(-jnp.abs(x)))


def _unit_lower_inverses(mats, row, col):
    n = mats[0].shape[0]

    def same_block(log2_size):
        return lax.shift_right_logical(row, log2_size) == lax.shift_right_logical(col, log2_size)

    eye = jnp.where(row == col, 1.0, 0.0)
    in_pair = same_block(1)
    xs = [eye - jnp.where(in_pair, a, 0.0) for a in mats]
    level = 1
    while (1 << level) < n:
        off_mask = same_block(level + 1) & jnp.logical_not(same_block(level))
        x_bfs = [_bf(x) for x in xs]
        halves = [_bf(_dot(x_bf, _bf(jnp.where(off_mask, a, 0.0)))) for x_bf, a in zip(x_bfs, mats)]
        xs = [x - _dot(half, x_bf) for x, half, x_bf in zip(xs, halves, x_bfs)]
        level += 1
    return xs


def _memkv_kernel(mem_ref, g_ref, wkv_ref, kg_ref, k_ref, v_ref):
    mn = _bf(_rms_scale(mem_ref[0]) * g_ref[...])
    kv = _dot(mn, wkv_ref[0])
    for h in range(XA_HEADS):
        sl = slice(h * XA_HEAD_DIM, (h + 1) * XA_HEAD_DIM)
        k_ref[0, 0, :, sl] = _bf(_rms_scale(kv[:, sl]) * kg_ref[0])
    v_ref[0, 0] = _bf(kv[:, XA_WIDTH:])


def _memkv(mem, norm_mem, wkv, k_g):
    b, m, d = mem.shape
    depth = wkv.shape[0]
    out = jax.ShapeDtypeStruct((depth, b, m, XA_WIDTH), BF16)
    return pl.pallas_call(
        _memkv_kernel,
        out_shape=(out, out),
        grid=(depth, b),
        in_specs=[
            pl.BlockSpec((1, m, d), lambda l, i: (i, 0, 0)),
            pl.BlockSpec((1, d), lambda l, i: (0, 0)),
            pl.BlockSpec((1, d, 2 * XA_WIDTH), lambda l, i: (l, 0, 0)),
            pl.BlockSpec((1, 1, XA_HEAD_DIM), lambda l, i: (l, 0, 0)),
        ],
        out_specs=(
            pl.BlockSpec((1, 1, m, XA_WIDTH), lambda l, i: (l, i, 0, 0)),
            pl.BlockSpec((1, 1, m, XA_WIDTH), lambda l, i: (l, i, 0, 0)),
        ),
        compiler_params=pltpu.CompilerParams(
            dimension_semantics=("arbitrary", "arbitrary"), vmem_limit_bytes=VMEM_LIMIT),
        name="memkv",
    )(mem, norm_mem.reshape(1, d), wkv, k_g.reshape(depth, 1, XA_HEAD_DIM))


def _proj_kernel(x_ref, g_ref, w_ref, wt_ref, wqbt_ref, wqit_ref, wk_ref, wvt_ref, qs_ref,
                 qkv_ref, z_ref, ug_ref, small_ref, kidx_ref, k_ref, qt_ref, qit_ref, vt_ref,
                 widxt_ref):
    h = _bf(_rms_scale(x_ref[0]) * g_ref[...])
    qkv_ref[0] = _bf(_dot(h, w_ref[:, _C_QKV[0]:_C_QKV[1]]))
    z_ref[0] = _bf(_dot(h, w_ref[:, _C_Z[0]:_C_Z[1]]))
    ug_ref[0] = _bf(_dot(h, w_ref[:, _C_UG[0]:_C_UG[1]]))
    small_ref[0] = _dot(h, w_ref[:, _C_SMALL[0]:_C_SMALL[1]])
    kidx_ref[0] = _bf(_dot(h, w_ref[:, _C_KIDX[0]:_C_KIDX[1]]))
    ckv = _bf(_dot(h, w_ref[:, _C_CKV[0]:_C_CKV[1]]))
    k_ref[0] = _bf(_rms_scale(_dot(ckv, wk_ref[...])))
    t = _dot_nt(wt_ref[...], h)
    cq_t = _bf(t[_R_CQ[0]:_R_CQ[1]])
    ckv_t = _bf(t[_R_CKV[0]:_R_CKV[1]])
    widxt_ref[0] = t[_R_WIDX[0]:_R_WIDX[1]]
    for hh in range(DSA_HEADS):
        qt = _dot(wqbt_ref[hh], cq_t)
        qt = _bf(qt * lax.rsqrt(jnp.mean(qt * qt, axis=0, keepdims=True) + EPS) * qs_ref[...])
        qit = _bf(_dot(wqit_ref[hh], cq_t))
        for blk in range(qt_ref.shape[1]):
            qt_ref[0, blk, :, hh * LANES:(hh + 1) * LANES] = qt[:, blk * LANES:(blk + 1) * LANES]
            qit_ref[0, blk, :, hh * LANES:(hh + 1) * LANES] = qit[:, blk * LANES:(blk + 1) * LANES]
    vt_ref[0, 0] = _bf(_dot(wvt_ref[...], ckv_t))


def _proj(x, g, w, wt, wqbt, wqit, wk, wvt, qs, tm):
    b, s, d = x.shape
    nt = s // tm
    tok = lambda n, dt=BF16: jax.ShapeDtypeStruct((b, s, n), dt)
    tok_spec = lambda n: pl.BlockSpec((1, tm, n), lambda i, j: (i, j, 0))
    full = lambda a: pl.BlockSpec(a.shape, lambda i, j, _n=a.ndim: (0,) * _n)
    out_shape = (
        tok(3 * GDN_WIDTH), tok(GDN_WIDTH), tok(2 * CONV_CH), tok(LANES, F32),
        tok(IDX_DIM), tok(DSA_HEAD_DIM),
        jax.ShapeDtypeStruct((b, s // LANES, DSA_HEAD_DIM, DSA_HEADS * LANES), BF16),
        jax.ShapeDtypeStruct((b, s // LANES, IDX_DIM, IDX_HEADS * LANES), BF16),
        jax.ShapeDtypeStruct((b, nt, DSA_HEAD_DIM, tm), BF16),
        jax.ShapeDtypeStruct((b, 8, s), F32),
    )
    head_spec = pl.BlockSpec((1, tm // LANES, DSA_HEAD_DIM, DSA_HEADS * LANES),
                             lambda i, j: (i, j, 0, 0))
    out_specs = (
        tok_spec(3 * GDN_WIDTH), tok_spec(GDN_WIDTH), tok_spec(2 * CONV_CH), tok_spec(LANES),
        tok_spec(IDX_DIM), tok_spec(DSA_HEAD_DIM), head_spec, head_spec,
        pl.BlockSpec((1, 1, DSA_HEAD_DIM, tm), lambda i, j: (i, j, 0, 0)),
        pl.BlockSpec((1, 8, tm), lambda i, j: (i, 0, j)),
    )
    return pl.pallas_call(
        _proj_kernel,
        out_shape=out_shape,
        grid=(b, nt),
        in_specs=[tok_spec(d), full(g), full(w), full(wt), full(wqbt), full(wqit), full(wk),
                  full(wvt), full(qs)],
        out_specs=out_specs,
        compiler_params=pltpu.CompilerParams(
            dimension_semantics=("parallel", "parallel"), vmem_limit_bytes=VMEM_LIMIT),
        name="proj",
    )(x, g, w, wt, wqbt, wqit, wk, wvt, qs)


def _gdn_kernel(qkv_ref, z_ref, small_ref, cw_ref, alog_ref, dt_ref, ng_ref, ltri_ref, y_ref,
                state_ref, halo_ref, cbuf_ref):
    c = GDN_CHUNK
    hd = GDN_HEAD_DIM
    ts = qkv_ref.shape[1]

    @pl.when(pl.program_id(1) == 0)
    def _():
        state_ref[...] = jnp.zeros_like(state_ref)
        halo_ref[...] = jnp.zeros_like(halo_ref)

    xin = qkv_ref[0].astype(F32)
    cbuf_ref[0:GDN_HALO, :] = halo_ref[...]
    cbuf_ref[GDN_HALO:GDN_HALO + ts, :] = xin
    halo_ref[...] = xin[ts - GDN_HALO:ts, :]
    base = GDN_HALO - (GDN_CONV - 1)
    acc = cw_ref[0:1, :] * cbuf_ref[base:base + ts, :]
    for j in range(1, GDN_CONV):
        acc = acc + cw_ref[j:j + 1, :] * cbuf_ref[base + j:base + j + ts, :]
    conv = _silu(acc)

    sm = small_ref[0]
    gate = -jnp.exp(alog_ref[...]) * _softplus(sm + dt_ref[...])
    beta_all = _sigmoid(sm)
    ltri = ltri_ref[...]

    row = lax.broadcasted_iota(jnp.int32, (c, c), 0)
    col = lax.broadcasted_iota(jnp.int32, (c, c), 1)
    tril = row >= col
    strict = row > col

    n_chunks = ts // c
    probs = [(ci, h) for ci in range(n_chunks) for h in range(GDN_HEADS)]
    decs, dec_ts = [], []
    for ci in range(n_chunks):
        g_hi, g_mid, g_lo = _split3(gate[ci * c:(ci + 1) * c])
        dec = _dot(ltri, g_hi) + _dot(ltri, g_mid) + _dot(ltri, g_lo)
        decs.append(dec)
        dec_ts.append(dec.T)

    def head_cols(ci, h, section):
        lo = section * GDN_WIDTH + h * hd
        return conv[ci * c:(ci + 1) * c, lo:lo + hd]

    qs, ks, kbs, vbs, d_cols, lasts, gammas = [], [], [], [], [], [], []
    for ci, h in probs:
        q = head_cols(ci, h, 0)
        k = head_cols(ci, h, 1)
        v = head_cols(ci, h, 2)
        q = q * lax.rsqrt(jnp.sum(q * q, axis=-1, keepdims=True) + EPS) * (hd ** -0.5)
        k = k * lax.rsqrt(jnp.sum(k * k, axis=-1, keepdims=True) + EPS)
        d_col = decs[ci][:, _L_AG + h:_L_AG + h + 1]
        d_row = dec_ts[ci][_L_AG + h:_L_AG + h + 1, :]
        beta = beta_all[ci * c:(ci + 1) * c, _L_BG + h:_L_BG + h + 1]
        qs.append(q)
        ks.append(k)
        kbs.append(k * beta)
        vbs.append(v * beta)
        d_cols.append(d_col)
        lasts.append(decs[ci][c - 1:c, _L_AG + h:_L_AG + h + 1])
        gammas.append(jnp.where(tril, jnp.exp(jnp.where(tril, d_col - d_row, 0.0)), 0.0))
    k_bfs = [_bf(k) for k in ks]
    a_mats = [jnp.where(strict, _dot_nt(_bf(kb), k_bf) * g, 0.0)
              for kb, k_bf, g in zip(kbs, k_bfs, gammas)]
    qks = [_bf(jnp.where(tril, _dot_nt(_bf(q), k_bf) * g, 0.0)) for q, k_bf, g in zip(qs, k_bfs, gammas)]
    t_mats = _unit_lower_inverses(a_mats, row, col)
    sols = [_dot(_bf(t), _bf(jnp.concatenate([vb, kb * jnp.exp(d)], axis=1)))
            for t, vb, kb, d in zip(t_mats, vbs, kbs, d_cols)]
    us = [sol[:, :hd] for sol in sols]
    ws = [_bf(sol[:, hd:]) for sol in sols]
    q_decs = [_bf(q * jnp.exp(d)) for q, d in zip(qs, d_cols)]
    k_dec_ts = [_bf((k * jnp.exp(last - d)).T) for k, last, d in zip(ks, lasts, d_cols)]
    decays = [jnp.exp(last) for last in lasts]

    states = [state_ref[h] for h in range(GDN_HEADS)]
    for ci in range(n_chunks):
        idx = [ci * GDN_HEADS + h for h in range(GDN_HEADS)]
        s_bfs = [_bf(s) for s in states]
        v_news = [_bf(us[i] - _dot(ws[i], s_bf)) for i, s_bf in zip(idx, s_bfs)]
        outs = [_dot(q_decs[i], s_bf) + _dot(qks[i], v_new)
                for i, s_bf, v_new in zip(idx, s_bfs, v_news)]
        states = [s * decays[i] + _dot(k_dec_ts[i], v_new)
                  for i, s, v_new in zip(idx, states, v_news)]
        for h, o in enumerate(outs):
            zh = z_ref[0, ci * c:(ci + 1) * c, h * hd:(h + 1) * hd].astype(F32)
            y_ref[0, ci * c:(ci + 1) * c, h * hd:(h + 1) * hd] = _bf(
                _rms_scale(o) * ng_ref[...] * _silu(zh))
    for h in range(GDN_HEADS):
        state_ref[h] = states[h]


def _gdn(qkv, z, small, conv_w, alog_row, dt_row, norm_g, ts):
    b, s, _ = qkv.shape
    c = GDN_CHUNK
    ltri = jnp.tril(jnp.ones((c, c), F32)).astype(BF16)
    tok_spec = lambda n: pl.BlockSpec((1, ts, n), lambda i, j: (i, j, 0))
    full = lambda a: pl.BlockSpec(a.shape, lambda i, j, _n=a.ndim: (0,) * _n)
    return pl.pallas_call(
        _gdn_kernel,
        out_shape=jax.ShapeDtypeStruct((b, s, GDN_WIDTH), BF16),
        grid=(b, s // ts),
        in_specs=[tok_spec(3 * GDN_WIDTH), tok_spec(GDN_WIDTH), tok_spec(LANES),
                  full(conv_w), full(alog_row), full(dt_row), full(norm_g), full(ltri)],
        out_specs=tok_spec(GDN_WIDTH),
        scratch_shapes=[
            pltpu.VMEM((GDN_HEADS, GDN_HEAD_DIM, GDN_HEAD_DIM), F32),
            pltpu.VMEM((GDN_HALO, 3 * GDN_WIDTH), F32),
            pltpu.VMEM((GDN_HALO + ts, 3 * GDN_WIDTH), F32),
        ],
        compiler_params=pltpu.CompilerParams(
            dimension_semantics=("parallel", "arbitrary"), vmem_limit_bytes=VMEM_LIMIT),
        name="gdn",
    )(qkv, z, small, conv_w, alog_row, dt_row, norm_g, ltri)


def _ordinal_to_float(u):
    key = u ^ jnp.int32(INT_MIN)
    bits = key ^ (lax.shift_right_arithmetic(key, 31) & jnp.int32(0x7FFFFFFF))
    f = lax.bitcast_convert_type(bits, F32)
    return jnp.where((bits < 0) & (f != f), -jnp.inf, f)


def _truncate_to_bf16(x):
    bits = lax.bitcast_convert_type(x, jnp.int32) & jnp.int32(-65536)
    return lax.bitcast_convert_type(bits, F32)


def _dsa_kernel(qt_ref, qit_ref, widxt_ref, kidx_ref, k_ref, vt_ref, ltri_ref, y_ref, score_ref,
                score_hi_ref, *, topk, n_slabs):
    tq = LANES
    kc = vt_ref.shape[3]
    last = n_slabs - 1
    q0 = (last * (kc // tq) + pl.program_id(1)) * tq
    w_all = widxt_ref[0] * (IDX_HEADS ** -0.5 * IDX_DIM ** -0.5)

    def slab_rows(c):
        return slice(c * kc, (c + 1) * kc)

    def head(x, h):
        return x[:, h * tq:(h + 1) * tq]

    def fold(x, op):
        part = op(x.reshape(x.shape[0] // _ACC_ROWS, _ACC_ROWS, x.shape[1]), axis=0)
        return op(part, axis=0, keepdims=True)

    qit = qit_ref[0, 0]
    for c in range(n_slabs):
        lg = _dot(kidx_ref[0, slab_rows(c), :], qit)
        sc = jnp.zeros((kc, tq), F32)
        for h in range(IDX_HEADS):
            sc = sc + w_all[h:h + 1, :] * jnp.maximum(head(lg, h), 0.0)
        if c == last:
            kpos = last * kc + lax.broadcasted_iota(jnp.int32, (kc, tq), 0)
            qpos = q0 + lax.broadcasted_iota(jnp.int32, (kc, tq), 1)
            sc = jnp.where(kpos <= qpos, sc, -jnp.inf)
        score_ref[c] = sc
        score_hi_ref[c] = _bf(_truncate_to_bf16(sc))

    kf = jnp.float32(topk)

    def count(pred):
        acc = jnp.zeros((_ACC_ROWS, tq), F32)
        for c in range(n_slabs):
            m = jnp.where(pred(score_ref[c]), 1.0, 0.0)
            acc = acc + jnp.sum(m.reshape(kc // _ACC_ROWS, _ACC_ROWS, tq), axis=0)
        return jnp.sum(acc, axis=0, keepdims=True)

    def count_hi(cand_hi):
        one, zero = jnp.ones((), BF16), jnp.zeros((), BF16)
        acc = jnp.zeros((_ACC_ROWS, tq), BF16)
        for c in range(n_slabs):
            m = jnp.where(score_hi_ref[c] >= cand_hi, one, zero)
            for r in range(0, kc, _ACC_ROWS):
                acc = acc + m[r:r + _ACC_ROWS]
        return jnp.sum(acc.astype(F32), axis=0, keepdims=True)

    def thr_hi_body(i, carry):
        t_u, cnt_t = carry
        cand_u = t_u | lax.shift_left(jnp.int32(1), 31 - i)
        cnt = count_hi(_bf(_truncate_to_bf16(_ordinal_to_float(cand_u))))
        return jnp.where(cnt >= kf, cand_u, t_u), jnp.where(cnt >= kf, cnt, cnt_t)

    def thr_body(i, carry):
        t_u, cnt_t = carry
        cand_u = t_u | lax.shift_left(jnp.int32(1), 31 - i)
        cand = _ordinal_to_float(cand_u)
        cnt = count(lambda sc: sc >= cand)
        return jnp.where(cnt >= kf, cand_u, t_u), jnp.where(cnt >= kf, cnt, cnt_t)

    assert n_slabs * kc // _ACC_ROWS <= 256
    carry = (jnp.zeros((1, tq), jnp.int32), jnp.full((1, tq), float(n_slabs * kc), F32))
    carry = lax.fori_loop(0, 16, thr_hi_body, carry)
    t_u, cnt_t = lax.fori_loop(16, 32, thr_body, carry)
    thr = _ordinal_to_float(t_u)
    has_thr = thr > -jnp.inf
    neg = jnp.float32(_MASKED)
    must_rank = jnp.max(jnp.where(has_thr, cnt_t - kf, 0.0)) > 0.0

    @pl.when(jnp.logical_not(must_rank))
    def _():
        thr_ge = jnp.where(has_thr, thr, jnp.finfo(F32).min)
        for c in range(n_slabs):
            score_ref[c] = jnp.where(score_ref[c] >= thr_ge, 0.0, neg)

    @pl.when(must_rank)
    def _():
        need = kf - count(lambda sc: sc > thr)
        thr_tie = jnp.where(has_thr, thr, jnp.nan)
        ltri = ltri_ref[...]
        sub = ltri.shape[0]
        seen = jnp.zeros((1, tq), F32)
        for c in range(n_slabs):
            for part in range(kc // sub):
                sc = score_ref[c, part * sub:(part + 1) * sub, :]
                tie = jnp.where(sc == thr_tie, 1.0, 0.0)
                rank = seen + _dot(ltri, _bf(tie))
                take = jnp.where(sc > thr, 1.0, jnp.where(rank <= need, tie, 0.0))
                score_ref[c, part * sub:(part + 1) * sub, :] = jnp.where(take > 0.5, 0.0, neg)
                seen = seen + jnp.sum(tie, axis=0, keepdims=True)

    qt = qt_ref[0, 0]
    logits = []
    m_all = jnp.full((1, DSA_HEADS * tq), neg, F32)
    for c in range(n_slabs):
        mask = score_ref[c]
        lg = _dot(k_ref[0, slab_rows(c), :], qt)
        lg = jnp.concatenate([head(lg, h) + mask for h in range(DSA_HEADS)], axis=1)
        logits.append(lg)
        m_all = jnp.maximum(m_all, fold(lg, jnp.max))
    l_all = jnp.zeros((1, DSA_HEADS * tq), F32)
    acc = jnp.zeros((DSA_HEAD_DIM, DSA_HEADS * tq), F32)
    for c in range(n_slabs):
        p = jnp.exp(logits[c] - m_all)
        l_all = l_all + fold(p, jnp.sum)
        acc = acc + _dot(vt_ref[0, c], _bf(p))
    out = acc * (1.0 / l_all)
    out_t = jnp.concatenate([head(out, h) for h in range(DSA_HEADS)], axis=0)
    y_ref[0] = _bf(out_t.T)


def _dsa(qt, qit, widxt, kidx, k, vt, tq):
    b, s, _ = k.shape
    total_slabs, kc = vt.shape[1], vt.shape[3]
    assert kc % _ACC_ROWS == 0 and kc % tq == 0 and tq == LANES
    topk = min(DSA_TOPK_MAX, s // 4)
    sub = min(kc, LANES)
    ltri = jnp.tril(jnp.ones((sub, sub), F32)).astype(BF16)
    per_group = kc // tq
    outs = []
    for n in range(1, total_slabs + 1):
        first = (n - 1) * per_group
        head_spec = pl.BlockSpec((1, 1, DSA_HEAD_DIM, DSA_HEADS * LANES),
                                 lambda i, j, _f=first: (i, _f + j, 0, 0))
        outs.append(pl.pallas_call(
            functools.partial(_dsa_kernel, topk=topk, n_slabs=n),
            out_shape=jax.ShapeDtypeStruct((b, kc, DSA_WIDTH), BF16),
            grid=(b, per_group),
            in_specs=[
                head_spec, head_spec,
                pl.BlockSpec((1, 8, tq), lambda i, j, _f=first: (i, 0, _f + j)),
                pl.BlockSpec((1, n * kc, IDX_DIM), lambda i, j: (i, 0, 0)),
                pl.BlockSpec((1, n * kc, DSA_HEAD_DIM), lambda i, j: (i, 0, 0)),
                pl.BlockSpec((1, n, DSA_HEAD_DIM, kc), lambda i, j: (i, 0, 0, 0)),
                pl.BlockSpec((sub, sub), lambda i, j: (0, 0)),
            ],
            out_specs=pl.BlockSpec((1, tq, DSA_WIDTH), lambda i, j: (i, j, 0)),
            scratch_shapes=[pltpu.VMEM((n, kc, tq), F32), pltpu.VMEM((n, kc, tq), BF16)],
            compiler_params=pltpu.CompilerParams(
                dimension_semantics=("parallel", "parallel"), vmem_limit_bytes=VMEM_LIMIT),
            name=f"dsa{n}",
        )(qt, qit, widxt, kidx, k, vt, ltri))
    return jnp.concatenate(outs, axis=1)


def _cconv_kernel(ug_ref, dw_ref, b_ref, lg_ref, lb_ref, y_ref, halo_ref, cbuf_ref, shift_ref):
    ts = ug_ref.shape[1]
    sl = 8

    @pl.when(pl.program_id(1) == 0)
    def _():
        halo_ref[...] = jnp.zeros_like(halo_ref)

    ug = ug_ref[0].astype(F32)
    glu = ug[:, :CONV_CH] * _sigmoid(ug[:, CONV_CH:])
    cbuf_ref[0:CONV_HALO, :] = halo_ref[...]
    cbuf_ref[CONV_HALO:CONV_HALO + ts, :] = glu
    halo_ref[...] = glu[ts - CONV_HALO:ts, :]
    span = shift_ref.shape[1]
    for r in range(1, sl):
        shift_ref[r - 1] = cbuf_ref[r:r + span, :]
    base = CONV_HALO - (CONV_WIDTH - 1)
    acc = None
    for j in range(CONV_WIDTH):
        off = base + j
        r, a = off % sl, (off // sl) * sl
        win = cbuf_ref[a:a + ts, :] if r == 0 else shift_ref[r - 1, a:a + ts, :]
        term = dw_ref[j:j + 1, :] * win
        acc = term if acc is None else acc + term
    acc = acc + b_ref[...]
    mu = jnp.mean(acc, axis=-1, keepdims=True)
    xc = acc - mu
    var = jnp.mean(xc * xc, axis=-1, keepdims=True)
    y = xc * lax.rsqrt(var + EPS) * lg_ref[...] + lb_ref[...]
    y_ref[0] = _bf(_silu(y))


def _cconv(ug, dw, dw_b, ln_g, ln_b, ts):
    b, s, _ = ug.shape
    full = lambda a: pl.BlockSpec(a.shape, lambda i, j, _n=a.ndim: (0,) * _n)
    return pl.pallas_call(
        _cconv_kernel,
        out_shape=jax.ShapeDtypeStruct((b, s, CONV_CH), BF16),
        grid=(b, s // ts),
        in_specs=[pl.BlockSpec((1, ts, 2 * CONV_CH), lambda i, j: (i, j, 0)),
                  full(dw), full(dw_b), full(ln_g), full(ln_b)],
        out_specs=pl.BlockSpec((1, ts, CONV_CH), lambda i, j: (i, j, 0)),
        scratch_shapes=[pltpu.VMEM((CONV_HALO, CONV_CH), F32),
                        pltpu.VMEM((CONV_HALO + ts, CONV_CH), F32),
                        pltpu.VMEM((7, CONV_HALO + ts - 8, CONV_CH), F32)],
        compiler_params=pltpu.CompilerParams(
            dimension_semantics=("parallel", "arbitrary"), vmem_limit_bytes=VMEM_LIMIT),
        name="cconv",
    )(ug, dw, dw_b, ln_g, ln_b)


def _post_kernel(x_ref, ya_ref, yb_ref, yc_ref, wout_ref, gc_ref, wq_ref, qg_ref, km_ref, vm_ref,
                 wo_ref, gm_ref, w1_ref, w2_ref, o_ref):
    x = x_ref[0]
    x = x + _dot(ya_ref[0], wout_ref[0:GDN_WIDTH, :])
    x = x + _dot(yb_ref[0], wout_ref[GDN_WIDTH:GDN_WIDTH + DSA_WIDTH, :])
    x = x + _dot(yc_ref[0], wout_ref[GDN_WIDTH + DSA_WIDTH:, :])

    hc = _bf(_rms_scale(x) * gc_ref[...])
    qx = _dot(hc, wq_ref[...])
    heads = []
    for h in range(XA_HEADS):
        sl = slice(h * XA_HEAD_DIM, (h + 1) * XA_HEAD_DIM)
        qh = _rms_scale(qx[:, sl]) * qg_ref[...] * (XA_HEAD_DIM ** -0.5)
        logits = _dot_nt(_bf(qh), km_ref[0, 0, :, sl])
        m = jnp.max(logits, axis=-1, keepdims=True)
        e = jnp.exp(logits - m)
        p = e * (1.0 / jnp.sum(e, axis=-1, keepdims=True))
        heads.append(_bf(_dot(_bf(p), vm_ref[0, 0, :, sl])))
    x = x + _dot(jnp.concatenate(heads, axis=1), wo_ref[...])

    hm = _bf(_rms_scale(x) * gm_ref[...])
    ff_chunk = D_MODEL
    acc = jnp.zeros_like(x)
    for cidx in range(D_FF // ff_chunk):
        a = jnp.maximum(_dot(hm, w1_ref[:, cidx * ff_chunk:(cidx + 1) * ff_chunk]), 0.0)
        acc = acc + _dot(_bf(a * a), w2_ref[cidx * ff_chunk:(cidx + 1) * ff_chunk, :])
    o_ref[0] = x + acc


def _post(x, ya, yb, yc, wout, g_cross, wq, q_g, kmem, vmem, layer, wo, g_mlp, w1, w2, tm):
    b, s, d = x.shape
    m = kmem.shape[2]
    tok_spec = lambda n: pl.BlockSpec((1, tm, n), lambda i, j: (i, j, 0))
    full = lambda a: pl.BlockSpec(a.shape, lambda i, j, _n=a.ndim: (0,) * _n,
                                  pipeline_mode=pl.Buffered(1))
    mem_spec = pl.BlockSpec((1, 1, m, XA_WIDTH), lambda i, j: (layer, i, 0, 0))
    return pl.pallas_call(
        _post_kernel,
        out_shape=jax.ShapeDtypeStruct((b, s, d), F32),
        grid=(b, s // tm),
        in_specs=[tok_spec(d), tok_spec(GDN_WIDTH), tok_spec(DSA_WIDTH), tok_spec(CONV_CH),
                  full(wout), full(g_cross), full(wq), full(q_g), mem_spec, mem_spec,
                  full(wo), full(g_mlp), full(w1), full(w2)],
        out_specs=tok_spec(d),
        compiler_params=pltpu.CompilerParams(
            dimension_semantics=("parallel", "parallel"), vmem_limit_bytes=VMEM_LIMIT),
        name="post",
    )(x, ya, yb, yc, wout, g_cross, wq, q_g, kmem, vmem, wo, g_mlp, w1, w2)


def _pack_w_in(w_in):
    depth, d, _ = w_in.shape
    sizes = (3 * GDN_WIDTH, GDN_WIDTH, GDN_HEADS, GDN_HEADS, DSA_Q_RANK, DSA_KV_RANK, IDX_DIM,
             IDX_HEADS, 2 * CONV_CH)
    parts, off = [], 0
    for n in sizes:
        parts.append(w_in[..., off:off + n])
        off += n
    qkv, z, a_g, b_g, c_q, c_kv, k_idx, w_idx, ug = parts
    small = jnp.concatenate(
        [a_g, b_g, jnp.zeros((depth, d, LANES - 2 * GDN_HEADS), F32)], axis=-1)
    packed = jnp.concatenate([qkv, z, ug, c_kv, small, k_idx], axis=-1).astype(BF16)
    tr = jnp.concatenate(
        [c_q, c_kv, w_idx,
         jnp.zeros((depth, d, N_TRANSPOSED - DSA_Q_RANK - DSA_KV_RANK - IDX_HEADS), F32)], axis=-1)
    return packed, jnp.swapaxes(tr, 1, 2).astype(BF16)


def _lane_row(vals, offset):
    depth, n = vals.shape
    row = jnp.zeros((depth, 1, LANES), F32)
    return row.at[:, 0, offset:offset + n].set(vals.astype(F32))


def _seq_tile(s, want):
    return want if s % want == 0 else s


def kernel(x, mem, norm_mix, w_in, gdn_conv, gdn_a_log, gdn_dt_bias, gdn_norm, dsa_w_qb, dsa_w_qi,
           dsa_w_kvb, dsa_q_norm, dsa_k_norm, conv_dw, conv_dw_b, conv_ln_g, conv_ln_b, w_out,
           norm_mem, norm_cross, xa_wq, xa_wkv, xa_q_norm, xa_k_norm, xa_wo, norm_mlp, mlp_w1,
           mlp_w2):
    depth = w_in.shape[0]
    b, s, d = x.shape
    assert d == D_MODEL and s % GDN_CHUNK == 0
    tm = _seq_tile(s, 512)
    tq = _seq_tile(s, 128)

    w_packed, w_tr = _pack_w_in(w_in)
    wqbt = jnp.transpose(dsa_w_qb.reshape(depth, DSA_Q_RANK, DSA_HEADS, DSA_HEAD_DIM), (0, 2, 3, 1)).astype(BF16)
    wqit = jnp.transpose(dsa_w_qi.reshape(depth, DSA_Q_RANK, IDX_HEADS, IDX_DIM), (0, 2, 3, 1)).astype(BF16)
    wk = dsa_w_kvb[:, :, :DSA_HEAD_DIM].astype(BF16)
    wvt = jnp.swapaxes(dsa_w_kvb[:, :, DSA_HEAD_DIM:], 1, 2).astype(BF16)
    q_scale = (dsa_q_norm * dsa_k_norm * (DSA_HEAD_DIM ** -0.5)).astype(F32)[:, :, None]
    alog_row = _lane_row(gdn_a_log, _L_AG)
    dt_row = _lane_row(gdn_dt_bias, _L_AG)

    kmem, vmem = _memkv(mem, norm_mem, xa_wkv.astype(BF16), xa_k_norm)

    for l in range(depth):
        qkv, z, ug, small, kidx, k, qt, qit, vt, widxt = _proj(
            x, norm_mix[l][None, :], w_packed[l], w_tr[l], wqbt[l], wqit[l], wk[l], wvt[l],
            q_scale[l], tm)
        y_a = _gdn(qkv, z, small, gdn_conv[l], alog_row[l], dt_row[l], gdn_norm[l][None, :], tm)
        y_b = _dsa(qt, qit, widxt, kidx, k, vt, tq)
        y_c = _cconv(ug, conv_dw[l], conv_dw_b[l][None, :], conv_ln_g[l][None, :],
                     conv_ln_b[l][None, :], tm)
        x = _post(x, y_a, y_b, y_c, w_out[l].astype(BF16), norm_cross[l][None, :],
                  xa_wq[l].astype(BF16), xa_q_norm[l][None, :], kmem, vmem, l,
                  xa_wo[l].astype(BF16), norm_mlp[l][None, :], mlp_w1[l].astype(BF16),
                  mlp_w2[l].astype(BF16), tm)
    return x
```
